```python
import jax, jax.numpy as jnp
from jax import lax
import numpy as np

D_MODEL = 1024
BATCH = 8
SEQ = 4096
DEPTH = 1

GDN_HEAD_DIM = 128
GDN_WIDTH = D_MODEL // 2
GDN_HEADS = GDN_WIDTH // GDN_HEAD_DIM
GDN_CONV = 4
CHUNK = 64
SC_WIDTH = D_MODEL - GDN_WIDTH
SC_GROUPS = 8
SC_CONV = 3
MIX_WIDTH = GDN_WIDTH + SC_WIDTH
IN_COLS = 4 * GDN_WIDTH + 2 * GDN_HEADS + 3 * SC_WIDTH
N_MEM = 256
XATTN_HEADS = 4
XATTN_HEAD_DIM = D_MODEL // XATTN_HEADS
D_FF = ((8 * D_MODEL + 3 * 256 - 1) // (3 * 256)) * 256
EPS = 1e-6

kernel_name = "hybrid_gdn_shortconv_memxattn_block"


def rms_norm(x, g):
    xf = x.astype(jnp.float32)
    y = xf * lax.rsqrt(jnp.mean(xf * xf, axis=-1, keepdims=True) + EPS)
    return (y * g.astype(jnp.float32)).astype(x.dtype)


def l2norm(x):
    return x * lax.rsqrt(jnp.sum(x * x, axis=-1, keepdims=True) + EPS)


def causal_depthwise_conv(x, w):
    K, C = w.shape
    return lax.conv_general_dilated(
        x, w[:, None, :].astype(x.dtype), window_strides=(1,), padding=[(K - 1, 0)],
        dimension_numbers=('NWC', 'WIO', 'NWC'), feature_group_count=C)


def gated_delta_rule_chunked(q, k, v, g, beta):
    B, S, H, dk = q.shape
    dv = v.shape[-1]
    N = S // CHUNK

    def to_chunks(t):
        t = t.reshape((B, N, CHUNK, H) + t.shape[3:])
        return jnp.moveaxis(t, 3, 1)

    q, k, v, g, beta = (to_chunks(t) for t in (q, k, v, g, beta))
    q = q * (dk ** -0.5)
    gc = jnp.cumsum(g, axis=-1)
    idx = jnp.arange(CHUNK)
    causal = idx[:, None] >= idx[None, :]
    strict = idx[:, None] > idx[None, :]
    diff = gc[..., :, None] - gc[..., None, :]
    decay = jnp.exp(jnp.where(causal, diff, -jnp.inf))
    k_beta = k * beta[..., None]
    A = jnp.where(strict, jnp.einsum('bhnid,bhnjd->bhnij', k_beta, k) * decay, 0.0)
    eye = jnp.eye(CHUNK, dtype=q.dtype)
    T = lax.linalg.triangular_solve(eye + A, jnp.broadcast_to(eye, A.shape),
                                    left_side=True, lower=True)
    u = jnp.einsum('bhnij,bhnjd->bhnid', T, v * beta[..., None])
    w = jnp.einsum('bhnij,bhnjd->bhnid', T, k_beta * jnp.exp(gc)[..., None])
    attn_qk = jnp.einsum('bhnid,bhnjd->bhnij', q, k) * decay
    q_dec = q * jnp.exp(gc)[..., None]
    g_last = gc[..., -1]
    k_dec = k * jnp.exp(g_last[..., None] - gc)[..., None]

    def step(state, inp):
        u_n, w_n, a_n, qd_n, kd_n, gl_n = inp
        v_new = u_n - jnp.einsum('bhcd,bhde->bhce', w_n, state)
        o = (jnp.einsum('bhcd,bhde->bhce', qd_n, state)
             + jnp.einsum('bhij,bhje->bhie', a_n, v_new))
        state = state * jnp.exp(gl_n)[..., None, None] + jnp.einsum('bhcd,bhce->bhde', kd_n, v_new)
        return state, o

    xs = tuple(jnp.moveaxis(t, 2, 0) for t in (u, w, attn_qk, q_dec, k_dec, g_last))
    state0 = jnp.zeros((B, H, dk, dv), jnp.float32)
    _, o = lax.scan(step, state0, xs)
    return jnp.transpose(o, (1, 0, 3, 2, 4)).reshape(B, S, H, dv)


def hybrid_mixer(xn, w_in, conv_qkv_w, a_log, dt_bias, gdn_norm_g, conv_sc_w, sc_norm_g, w_out):
    B, S, _ = xn.shape
    f32 = jnp.float32
    proj = xn @ w_in
    cuts = [3 * GDN_WIDTH, 4 * GDN_WIDTH, 4 * GDN_WIDTH + GDN_HEADS,
            4 * GDN_WIDTH + 2 * GDN_HEADS, 4 * GDN_WIDTH + 2 * GDN_HEADS + SC_WIDTH,
            4 * GDN_WIDTH + 2 * GDN_HEADS + 2 * SC_WIDTH]
    qkv, z, a, b, sc_b, sc_c, sc_h = jnp.split(proj, cuts, axis=-1)

    qkv = jax.nn.silu(causal_depthwise_conv(qkv, conv_qkv_w))
    q, k, v = jnp.split(qkv, 3, axis=-1)
    heads = lambda t: t.reshape(B, S, GDN_HEADS, GDN_HEAD_DIM).astype(f32)
    q, k, v = l2norm(heads(q)), l2norm(heads(k)), heads(v)
    g = -jnp.exp(a_log.astype(f32)) * jax.nn.softplus(a.astype(f32) + dt_bias.astype(f32))
    beta = jax.nn.sigmoid(b.astype(f32))
    o = gated_delta_rule_chunked(q, k, v, g, beta)
    o = rms_norm(o, gdn_norm_g) * jax.nn.silu(heads(z))
    o_gdn = o.reshape(B, S, GDN_WIDTH).astype(xn.dtype)

    y = sc_b * causal_depthwise_conv(sc_c * sc_h, conv_sc_w)
    y = rms_norm(y.reshape(B, S, SC_GROUPS, SC_WIDTH // SC_GROUPS),
                 sc_norm_g.reshape(SC_GROUPS, SC_WIDTH // SC_GROUPS)).reshape(B, S, SC_WIDTH)

    return jnp.concatenate([o_gdn, y.astype(xn.dtype)], axis=-1) @ w_out


def memory_cross_attention(hn, memn, w_xq, w_xk, w_xv, w_xo):
    B, S, _ = hn.shape
    M = memn.shape[1]
    q = (hn @ w_xq).reshape(B, S, XATTN_HEADS, XATTN_HEAD_DIM)
    k = (memn @ w_xk).reshape(B, M, XATTN_HEADS, XATTN_HEAD_DIM)
    v = (memn @ w_xv).reshape(B, M, XATTN_HEADS, XATTN_HEAD_DIM)
    s = jnp.einsum('bshd,bmhd->bhsm', q, k).astype(jnp.float32) * (XATTN_HEAD_DIM ** -0.5)
    p = jax.nn.softmax(s, axis=-1).astype(v.dtype)
    o = jnp.einsum('bhsm,bmhd->bshd', p, v).reshape(B, S, XATTN_HEADS * XATTN_HEAD_DIM)
    return o @ w_xo


def swiglu(hn, w_gate, w_up, w_down):
    return (jax.nn.silu(hn @ w_gate) * (hn @ w_up)) @ w_down


def setup_inputs(seed: int = 0) -> dict:
    key = jax.random.key(seed)
    ks = jax.random.split(key, 24)
    f32 = jnp.float32
    L = DEPTH

    def dense(k, fan_in, shape):
        return jax.random.normal(k, shape, f32) * (fan_in ** -0.5)

    def gain(k, n):
        return 1.0 + 0.02 * jax.random.normal(k, (L, n), f32)

    dt = jnp.exp(jax.random.uniform(ks[5], (L, GDN_HEADS), f32, np.log(1e-3), np.log(1e-1)))
    return {
        "x": jax.random.normal(ks[0], (BATCH, SEQ, D_MODEL), f32),
        "mem": jax.random.normal(ks[1], (BATCH, N_MEM, D_MODEL), f32),
        "norm_mix_g": gain(ks[2], D_MODEL),
        "w_in": dense(ks[3], D_MODEL, (L, D_MODEL, IN_COLS)),
        "conv_qkv_w": dense(ks[4], GDN_CONV, (L, GDN_CONV, 3 * GDN_WIDTH)),
        "a_log": jnp.log(jax.random.uniform(ks[6], (L, GDN_HEADS), f32, 1.0, 16.0)),
        "dt_bias": dt + jnp.log(-jnp.expm1(-dt)),
        "gdn_norm_g": gain(ks[7], GDN_HEAD_DIM),
        "conv_sc_w": dense(ks[8], SC_CONV, (L, SC_CONV, SC_WIDTH)),
        "sc_norm_g": gain(ks[9], SC_WIDTH),
        "w_out": dense(ks[10], MIX_WIDTH, (L, MIX_WIDTH, D_MODEL)),
        "norm_x_g": gain(ks[11], D_MODEL),
        "norm_mem_g": gain(ks[12], D_MODEL),
        "w_xq": dense(ks[13], D_MODEL, (L, D_MODEL, D_MODEL)),
        "w_xk": dense(ks[14], D_MODEL, (L, D_MODEL, D_MODEL)),
        "w_xv": dense(ks[15], D_MODEL, (L, D_MODEL, D_MODEL)),
        "w_xo": dense(ks[16], D_MODEL, (L, D_MODEL, D_MODEL)),
        "norm_ffn_g": gain(ks[17], D_MODEL),
        "w_gate": dense(ks[18], D_MODEL, (L, D_MODEL, D_FF)),
        "w_up": dense(ks[19], D_MODEL, (L, D_MODEL, D_FF)),
        "w_down": dense(ks[20], D_FF, (L, D_FF, D_MODEL)),
        "norm_final_g": 1.0 + 0.02 * jax.random.normal(ks[21], (D_MODEL,), f32),
    }


def reference(x, mem, norm_mix_g, w_in, conv_qkv_w, a_log, dt_bias, gdn_norm_g, conv_sc_w,
              sc_norm_g, w_out, norm_x_g, norm_mem_g, w_xq, w_xk, w_xv, w_xo, norm_ffn_g,
              w_gate, w_up, w_down, norm_final_g):
    h = x
    for l in range(DEPTH):
        h = h + hybrid_mixer(rms_norm(h, norm_mix_g[l]), w_in[l], conv_qkv_w[l], a_log[l],
                             dt_bias[l], gdn_norm_g[l], conv_sc_w[l], sc_norm_g[l], w_out[l])
        h = h + memory_cross_attention(rms_norm(h, norm_x_g[l]), rms_norm(mem, norm_mem_g[l]),
                                       w_xq[l], w_xk[l], w_xv[l], w_xo[l])
        h = h + swiglu(rms_norm(h, norm_ffn_g[l]), w_gate[l], w_up[l], w_down[l])
    return rms_norm(h, norm_final_g)
```

```python
import functools

import jax
import jax.numpy as jnp
from jax import lax
from jax.experimental import pallas as pl
from jax.experimental.pallas import tpu as pltpu

EPS = 1e-6
BF = jnp.bfloat16
F32 = jnp.float32

GDN_HEAD_DIM = 128
CHUNK = 64
SC_GROUP = 64
XATTN_HEADS = 4
HALO = 8
VMEM_LIMIT = 56 * 1024 * 1024


def _dot(a, b):
    return jnp.dot(a, b, preferred_element_type=F32)


def _dot_nt(a, b):
    return lax.dot_general(a, b, (((1,), (1,)), ((), ())), preferred_element_type=F32)


def _dot_tn(a, b):
    return lax.dot_general(a, b, (((0,), (0,)), ((), ())), preferred_element_type=F32)


def _rmsnorm(x, g):
    ms = jnp.mean(x * x, axis=-1, keepdims=True)
    return x * lax.rsqrt(ms + EPS) * g


def _sigmoid(x):
    return 1.0 / (1.0 + jnp.exp(-x))


def _silu(x):
    return x * _sigmoid(x)


def _softplus(x):
    return jnp.maximum(x, 0.0) + jnp.log(1.0 + jnp.exp(-jnp.abs(x)))


def _inproj_body(x_ref, g_ref, w_ref, wab_ref, proj_ref, ab_ref, *, col_chunk):
    xn = _rmsnorm(x_ref[...], g_ref[...]).astype(BF)
    n = w_ref.shape[1]
    for c in range(0, n, col_chunk):
        proj_ref[:, c:c + col_chunk] = _dot(xn, w_ref[:, c:c + col_chunk]).astype(BF)
    ab_ref[...] = _dot(xn, wab_ref[...])


def _in_proj(x2, g, w_main, w_ab, *, tm):
    t, d = x2.shape
    n = w_main.shape[1]
    return pl.pallas_call(
        functools.partial(_inproj_body, col_chunk=512),
        grid=(t // tm,),
        in_specs=[
            pl.BlockSpec((tm, d), lambda i: (i, 0)),
            pl.BlockSpec((1, d), lambda i: (0, 0)),
            pl.BlockSpec((d, n), lambda i: (0, 0)),
            pl.BlockSpec((d, 128), lambda i: (0, 0)),
        ],
        out_specs=[
            pl.BlockSpec((tm, n), lambda i: (i, 0)),
            pl.BlockSpec((tm, 128), lambda i: (i, 0)),
        ],
        out_shape=[
            jax.ShapeDtypeStruct((t, n), BF),
            jax.ShapeDtypeStruct((t, 128), F32),
        ],
        compiler_params=pltpu.CompilerParams(
            dimension_semantics=("arbitrary",), vmem_limit_bytes=VMEM_LIMIT),
        name="in_proj",
    )(x2, g, w_main, w_ab)


def _memkv_body(mem_ref, g_ref, wk_ref, wv_ref, k_ref, v_ref):
    mn = _rmsnorm(mem_ref[0], g_ref[...]).astype(BF)
    k_ref[0] = _dot(mn, wk_ref[...]).astype(BF)
    v_ref[0] = _dot(mn, wv_ref[...]).astype(BF)


def _mem_kv(mem, g, wk, wv):
    b, m, d = mem.shape
    return pl.pallas_call(
        _memkv_body,
        grid=(b,),
        in_specs=[
            pl.BlockSpec((1, m, d), lambda i: (i, 0, 0)),
            pl.BlockSpec((1, d), lambda i: (0, 0)),
            pl.BlockSpec((d, d), lambda i: (0, 0)),
            pl.BlockSpec((d, d), lambda i: (0, 0)),
        ],
        out_specs=[
            pl.BlockSpec((1, m, d), lambda i: (i, 0, 0)),
            pl.BlockSpec((1, m, d), lambda i: (i, 0, 0)),
        ],
        out_shape=[jax.ShapeDtypeStruct((b, m, d), BF)] * 2,
        compiler_params=pltpu.CompilerParams(
            dimension_semantics=("arbitrary",), vmem_limit_bytes=VMEM_LIMIT),
        name="mem_kv",
    )(mem, g, wk, wv)


def _causal_conv(x, hist, w_ref, col):
    k = w_ref.shape[0]
    acc = x * w_ref[k - 1:k, col]
    rows = lax.broadcasted_iota(jnp.int32, (HALO, x.shape[1]), 0)
    for sh in range(1, k):
        xr = pltpu.roll(x, sh, axis=0)
        hr = pltpu.roll(hist, sh, axis=0)
        first = jnp.where(rows < sh, hr, xr[0:HALO])
        xs = jnp.concatenate([first, xr[HALO:]], axis=0)
        acc = acc + xs * w_ref[k - 1 - sh:k - sh, col]
    return acc


def _blockdiag2(a, b):
    za = jnp.zeros_like(a)
    return jnp.concatenate(
        [jnp.concatenate([a, za], axis=1), jnp.concatenate([za, b], axis=1)], axis=0)


def _mixer_body(qkv_ref, z_ref, scb_ref, scc_ref, sch_ref, ab_ref,
                cw_ref, alog_ref, dtb_ref, gng_ref, scw_ref, scg_ref,
                mix_ref,
                halo_qkv, halo_sc, state, qkv_s, *, ts, heads):
    hd = GDN_HEAD_DIM
    gw = heads * hd
    nc = ts // CHUNK

    @pl.when(pl.program_id(1) == 0)
    def _():
        halo_qkv[...] = jnp.zeros_like(halo_qkv)
        halo_sc[...] = jnp.zeros_like(halo_sc)
        state[...] = jnp.zeros_like(state)

    for blk in range(3 * heads):
        col = slice(blk * hd, (blk + 1) * hd)
        x = qkv_ref[:, col].astype(F32)
        y = _silu(_causal_conv(x, halo_qkv[:, col], cw_ref, col))
        halo_qkv[:, col] = x[ts - HALO:ts]
        if blk < 2 * heads:
            y = y * lax.rsqrt(jnp.sum(y * y, axis=-1, keepdims=True) + EPS)
        if blk < heads:
            y = y * (hd ** -0.5)
        qkv_s[:, col] = y

    ab = ab_ref[...]
    g_all = -jnp.exp(alog_ref[...]) * _softplus(ab + dtb_ref[...])
    beta_all = _sigmoid(ab)
    ri = lax.broadcasted_iota(jnp.int32, (ts, ts), 0)
    ci = lax.broadcasted_iota(jnp.int32, (ts, ts), 1)
    ltri = jnp.where((ci <= ri) & (ri // CHUNK == ci // CHUNK), 1.0, 0.0).astype(F32)
    gc_all = jnp.dot(ltri, g_all, preferred_element_type=F32,
                     precision=lax.Precision.HIGHEST)

    r64 = lax.broadcasted_iota(jnp.int32, (CHUNK, 2 * CHUNK), 0)
    l64 = lax.broadcasted_iota(jnp.int32, (CHUNK, 2 * CHUNK), 1)
    lm = l64 % CHUNK
    causal = r64 >= lm
    strict = r64 > lm
    eye_p = r64 == lm
    lane_lo = l64 < CHUNK
    rb = lax.broadcasted_iota(jnp.int32, (2 * CHUNK, 2 * CHUNK), 0)
    lb = lax.broadcasted_iota(jnp.int32, (2 * CHUNK, 2 * CHUNK), 1)
    blockmask = (rb // CHUNK) == (lb // CHUNK)

    def bd(m):
        return jnp.where(blockmask, jnp.concatenate([m, m], axis=0), 0.0).astype(BF)

    for c in range(nc):
        rows = slice(c * CHUNK, (c + 1) * CHUNK)
        last = slice((c + 1) * CHUNK - 1, (c + 1) * CHUNK)
        q_h, k_h, vb_h, kb_h, kbg_h, qd_h, kd_h, gcol_h, glast_h = [], [], [], [], [], [], [], [], []
        for h in range(heads):
            q = qkv_s[rows, h * hd:(h + 1) * hd]
            k = qkv_s[rows, gw + h * hd:gw + (h + 1) * hd]
            v = qkv_s[rows, 2 * gw + h * hd:2 * gw + (h + 1) * hd]
            gcol = gc_all[rows, h:h + 1]
            bcol = beta_all[rows, heads + h:heads + h + 1]
            glast = gc_all[last, h:h + 1]
            egc = jnp.exp(gcol)
            kb = k * bcol
            q_h.append(q)
            k_h.append(k)
            vb_h.append(v * bcol)
            kb_h.append(kb)
            kbg_h.append(kb * egc)
            qd_h.append(q * egc)
            kd_h.append(k * jnp.exp(glast - gcol))
            gcol_h.append(gcol)
            glast_h.append(glast)

        for p in range(heads // 2):
            h0, h1 = 2 * p, 2 * p + 1
            gmat = jnp.where(lane_lo, jnp.broadcast_to(gcol_h[h0], (CHUNK, 2 * CHUNK)),
                             jnp.broadcast_to(gcol_h[h1], (CHUNK, 2 * CHUNK)))
            grow = jnp.sum(jnp.where(eye_p, gmat, 0.0), axis=0, keepdims=True)
            decay = jnp.exp(jnp.where(causal, gmat - grow, -jnp.inf))
            lhs = jnp.concatenate([
                jnp.concatenate([kb_h[h0], kb_h[h1]], axis=1),
                jnp.concatenate([q_h[h0], q_h[h1]], axis=1)], axis=0).astype(BF)
            kk = _dot_nt(lhs, _blockdiag2(k_h[h0], k_h[h1]).astype(BF))
            a_mat = jnp.where(strict, kk[:CHUNK] * decay, 0.0)
            attn = kk[CHUNK:] * decay
            n_mat = -a_mat
            b_mat = _dot(a_mat.astype(BF), bd(a_mat))
            for _ in range(4):
                r = _dot(jnp.concatenate([n_mat, b_mat], axis=0).astype(BF), bd(b_mat))
                n_mat = n_mat + b_mat + r[:CHUNK]
                b_mat = r[CHUNK:]
            n_mat = n_mat + b_mat + _dot(n_mat.astype(BF), bd(b_mat))
            t_mat = jnp.where(eye_p, 1.0, 0.0) + n_mat
            rhs = jnp.concatenate([_blockdiag2(vb_h[h0], vb_h[h1]),
                                   _blockdiag2(kbg_h[h0], kbg_h[h1])], axis=1).astype(BF)
            uw = _dot(t_mat.astype(BF), rhs)
            vn = []
            qs = []
            for j, h in enumerate((h0, h1)):
                s_h = state[h]
                r1 = _dot(jnp.concatenate([uw[:, 2 * hd + j * hd:2 * hd + (j + 1) * hd], qd_h[h]],
                                          axis=0).astype(BF), s_h.astype(BF))
                vn.append(uw[:, j * hd:(j + 1) * hd] - r1[:CHUNK])
                qs.append(r1[CHUNK:])
            av = _dot(attn.astype(BF), _blockdiag2(vn[0], vn[1]).astype(BF))
            for j, h in enumerate((h0, h1)):
                o = qs[j] + av[:, j * hd:(j + 1) * hd]
                state[h] = state[h] * jnp.exp(glast_h[h]) + _dot_tn(kd_h[h].astype(BF), vn[j].astype(BF))
                zc = z_ref[rows, h * hd:(h + 1) * hd].astype(F32)
                on = _rmsnorm(o, gng_ref[...]) * _silu(zc)
                mix_ref[rows, h * hd:(h + 1) * hd] = on.astype(BF)

    scw = scb_ref.shape[1]
    gi = lax.broadcasted_iota(jnp.int32, (hd, hd), 0) // SC_GROUP
    gj = lax.broadcasted_iota(jnp.int32, (hd, hd), 1) // SC_GROUP
    gavg = jnp.where(gi == gj, 1.0 / SC_GROUP, 0.0).astype(F32)
    for blk in range(scw // hd):
        col = slice(blk * hd, (blk + 1) * hd)
        ch = scc_ref[:, col].astype(F32) * sch_ref[:, col].astype(F32)
        y = scb_ref[:, col].astype(F32) * _causal_conv(ch, halo_sc[:, col], scw_ref, col)
        halo_sc[:, col] = ch[ts - HALO:ts]
        ms = jnp.dot(y * y, gavg, preferred_element_type=F32, precision=lax.Precision.HIGHEST)
        yn = y * lax.rsqrt(ms + EPS) * scg_ref[:, col]
        mix_ref[:, gw + blk * hd:gw + (blk + 1) * hd] = yn.astype(BF)


def _mixer(proj, ab, conv_qkv_w, alog_row, dtb_row, gdn_norm_g, conv_sc_w, sc_norm_g,
           *, batch, seq, heads, ts):
    hd = GDN_HEAD_DIM
    gw = heads * hd
    scw = conv_sc_w.shape[1]
    nblk = seq // ts
    assert 3 * gw % scw == 0 and gw == scw

    def row(b, s):
        return b * nblk + s

    in_specs = [
        pl.BlockSpec((ts, 3 * gw), lambda b, s: (row(b, s), 0)),
        pl.BlockSpec((ts, gw), lambda b, s: (row(b, s), 3)),
        pl.BlockSpec((ts, scw), lambda b, s: (row(b, s), 4)),
        pl.BlockSpec((ts, scw), lambda b, s: (row(b, s), 5)),
        pl.BlockSpec((ts, scw), lambda b, s: (row(b, s), 6)),
        pl.BlockSpec((ts, 128), lambda b, s: (row(b, s), 0)),
        pl.BlockSpec(conv_qkv_w.shape, lambda b, s: (0, 0)),
        pl.BlockSpec((1, 128), lambda b, s: (0, 0)),
        pl.BlockSpec((1, 128), lambda b, s: (0, 0)),
        pl.BlockSpec((1, hd), lambda b, s: (0, 0)),
        pl.BlockSpec(conv_sc_w.shape, lambda b, s: (0, 0)),
        pl.BlockSpec((1, scw), lambda b, s: (0, 0)),
    ]
    return pl.pallas_call(
        functools.partial(_mixer_body, ts=ts, heads=heads),
        grid=(batch, nblk),
        in_specs=in_specs,
        out_specs=pl.BlockSpec((ts, gw + scw), lambda b, s: (row(b, s), 0)),
        out_shape=jax.ShapeDtypeStruct((batch * seq, gw + scw), BF),
        scratch_shapes=[
            pltpu.VMEM((HALO, 3 * gw), F32),
            pltpu.VMEM((HALO, scw), F32),
            pltpu.VMEM((heads, hd, hd), F32),
            pltpu.VMEM((ts, 3 * gw), F32),
        ],
        compiler_params=pltpu.CompilerParams(
            dimension_semantics=("arbitrary", "arbitrary"), vmem_limit_bytes=VMEM_LIMIT),
        name="mixer",
    )(proj, proj, proj, proj, proj, ab, conv_qkv_w, alog_row, dtb_row, gdn_norm_g,
      conv_sc_w, sc_norm_g)


def _xattn_body(mix_ref, x_ref, wout_ref, gx_ref, wq_ref, k_ref, v_ref, wo_ref, h2_ref, *, heads):
    d = x_ref.shape[1]
    dh = d // heads
    h1 = x_ref[...] + _dot(mix_ref[...], wout_ref[...])
    hn = _rmsnorm(h1, gx_ref[...]).astype(BF)
    q = (_dot(hn, wq_ref[...]) * (dh ** -0.5)).astype(BF)
    outs = []
    for h in range(heads):
        col = slice(h * dh, (h + 1) * dh)
        s = _dot_nt(q[:, col], k_ref[0, :, col])
        m = jnp.max(s, axis=-1, keepdims=True)
        p = jnp.exp(s - m)
        p = p * (1.0 / jnp.sum(p, axis=-1, keepdims=True))
        outs.append(_dot(p.astype(BF), v_ref[0, :, col]))
    o = jnp.concatenate(outs, axis=1).astype(BF)
    h2_ref[...] = h1 + _dot(o, wo_ref[...])


def _xattn(mix, x2, w_out, gx, w_xq, kmem, vmem, w_xo, *, batch, seq, tm):
    t, d = x2.shape
    m = kmem.shape[1]
    nblk = seq // tm
    const = lambda b, s: (0, 0)
    return pl.pallas_call(
        functools.partial(_xattn_body, heads=XATTN_HEADS),
        grid=(batch, nblk),
        in_specs=[
            pl.BlockSpec((tm, d), lambda b, s: (b * nblk + s, 0)),
            pl.BlockSpec((tm, d), lambda b, s: (b * nblk + s, 0)),
            pl.BlockSpec((d, d), const),
            pl.BlockSpec((1, d), const),
            pl.BlockSpec((d, d), const),
            pl.BlockSpec((1, m, d), lambda b, s: (b, 0, 0)),
            pl.BlockSpec((1, m, d), lambda b, s: (b, 0, 0)),
            pl.BlockSpec((d, d), const),
        ],
        out_specs=pl.BlockSpec((tm, d), lambda b, s: (b * nblk + s, 0)),
        out_shape=jax.ShapeDtypeStruct((t, d), F32),
        compiler_params=pltpu.CompilerParams(
            dimension_semantics=("arbitrary", "arbitrary"), vmem_limit_bytes=VMEM_LIMIT),
        name="xattn",
    )(mix, x2, w_out, gx, w_xq, kmem, vmem, w_xo)


def _ffn_body(h_ref, gf_ref, wg_ref, wu_ref, wd_ref, gfin_ref, out_ref, acc_ref, *, ff_chunk):
    h2 = h_ref[...]
    hn = _rmsnorm(h2, gf_ref[...]).astype(BF)
    dff = wg_ref.shape[1]
    acc_ref[...] = h2
    for c in range(0, dff, ff_chunk):
        w = min(ff_chunk, dff - c)
        gate = _dot(hn, wg_ref[:, c:c + w])
        up = _dot(hn, wu_ref[:, c:c + w])
        act = (_silu(gate) * up).astype(BF)
        acc_ref[...] += _dot(act, wd_ref[c:c + w, :])
    out_ref[...] = _rmsnorm(acc_ref[...], gfin_ref[...])


def _ffn(h2, gf, wg, wu, wd, gfin, *, tm):
    t, d = h2.shape
    dff = wg.shape[1]
    const = lambda i: (0, 0)
    resident = dict(pipeline_mode=pl.Buffered(1))
    return pl.pallas_call(
        functools.partial(_ffn_body, ff_chunk=512),
        grid=(t // tm,),
        in_specs=[
            pl.BlockSpec((tm, d), lambda i: (i, 0)),
            pl.BlockSpec((1, d), const),
            pl.BlockSpec((d, dff), const, **resident),
            pl.BlockSpec((d, dff), const, **resident),
            pl.BlockSpec((dff, d), const, **resident),
            pl.BlockSpec((1, d), const),
        ],
        out_specs=pl.BlockSpec((tm, d), lambda i: (i, 0)),
        out_shape=jax.ShapeDtypeStruct((t, d), F32),
        scratch_shapes=[pltpu.VMEM((tm, d), F32)],
        compiler_params=pltpu.CompilerParams(
            dimension_semantics=("arbitrary",), vmem_limit_bytes=VMEM_LIMIT),
        name="ffn",
    )(h2, gf, wg, wu, wd, gfin)


def _pad_lanes(v, n=128):
    return jnp.zeros((1, n), F32).at[0, :v.shape[0]].set(v.astype(F32))


def kernel(x, mem, norm_mix_g, w_in, conv_qkv_w, a_log, dt_bias, gdn_norm_g, conv_sc_w, sc_norm_g,
           w_out, norm_x_g, norm_mem_g, w_xq, w_xk, w_xv, w_xo, norm_ffn_g, w_gate, w_up, w_down,
           norm_final_g):
    batch, seq, d = x.shape
    depth = w_in.shape[0]
    heads = a_log.shape[1]
    gw = heads * GDN_HEAD_DIM
    scw = conv_sc_w.shape[2]
    h = x.reshape(batch * seq, d)
    row = lambda v: v.reshape(1, -1).astype(F32)
    for l in range(depth):
        c_gate = 4 * gw
        w_main = jnp.concatenate([w_in[l][:, :c_gate], w_in[l][:, c_gate + 2 * heads:]], axis=1).astype(BF)
        w_ab = jnp.zeros((d, 128), F32).at[:, :2 * heads].set(w_in[l][:, c_gate:c_gate + 2 * heads]).astype(BF)
        proj, ab = _in_proj(h, row(norm_mix_g[l]), w_main, w_ab, tm=512)
        mix = _mixer(proj, ab, conv_qkv_w[l].astype(F32), _pad_lanes(a_log[l]), _pad_lanes(dt_bias[l]),
                     row(gdn_norm_g[l]), conv_sc_w[l].astype(F32), row(sc_norm_g[l]),
                     batch=batch, seq=seq, heads=heads, ts=256)
        kmem, vmem = _mem_kv(mem, row(norm_mem_g[l]), w_xk[l].astype(BF), w_xv[l].astype(BF))
        h = _xattn(mix, h, w_out[l].astype(BF), row(norm_x_g[l]), w_xq[l].astype(BF), kmem, vmem,
                   w_xo[l].astype(BF), batch=batch, seq=seq, tm=512)
        last = l == depth - 1
        assert last, "final norm is fused into the last layer's FFN"
        h = _ffn(h, row(norm_ffn_g[l]), w_gate[l].astype(BF), w_up[l].astype(BF), w_down[l].astype(BF),
                 row(norm_final_g), tm=512)
    return h.reshape(batch, seq, d)
```

```python
import functools

import jax
import jax.numpy as jnp
from jax import lax
from jax.experimental import pallas as pl
from jax.experimental.pallas import tpu as pltpu

EPS = 1e-6
BF = jnp.bfloat16
F32 = jnp.float32

LANES = 128
GDN_HEAD_DIM = 128
CHUNK = 64
SC_GROUP = 64
XATTN_HEADS = 4
HALO = 8
VMEM_LIMIT = 56 * 1024 * 1024


def _dot(a, b):
    return jnp.dot(a, b, preferred_element_type=F32)


def _dot_nt(a, b):
    return lax.dot_general(a, b, (((1,), (1,)), ((), ())), preferred_element_type=F32)


def _dot_tn(a, b):
    return lax.dot_general(a, b, (((0,), (0,)), ((), ())), preferred_element_type=F32)


def _rmsnorm(x, g):
    ms = jnp.mean(x * x, axis=-1, keepdims=True)
    return x * lax.rsqrt(ms + EPS) * g


def _sigmoid(x):
    return 1.0 / (1.0 + jnp.exp(-x))


def _silu(x):
    return x * _sigmoid(x)


def _softplus(x):
    return jnp.maximum(x, 0.0) + jnp.log(1.0 + jnp.exp(-jnp.abs(x)))


def _sum(xs):
    return functools.reduce(lambda a, b: a + b, xs)


def _split_bf16(x, terms):
    out = []
    for _ in range(terms - 1):
        hi = x.astype(BF)
        out.append(hi)
        x = x - hi.astype(F32)
    out.append(x.astype(BF))
    return out


def _inproj_body(x_ref, g_ref, w_ref, wab_ref, proj_ref, ab_ref, *, col_chunk):
    xn = _rmsnorm(x_ref[...], g_ref[...]).astype(BF)
    n = w_ref.shape[1]
    for c in range(0, n, col_chunk):
        proj_ref[:, c:c + col_chunk] = _dot(xn, w_ref[:, c:c + col_chunk]).astype(BF)
    ab_ref[...] = _dot(xn, wab_ref[0])


def _in_proj(x2, g, w_main, w_ab, *, tm, seq):
    t, d = x2.shape
    n = w_main.shape[1]
    per_batch = seq // tm
    return pl.pallas_call(
        functools.partial(_inproj_body, col_chunk=512),
        grid=(t // tm,),
        in_specs=[
            pl.BlockSpec((tm, d), lambda i: (i, 0)),
            pl.BlockSpec((1, d), lambda i: (0, 0)),
            pl.BlockSpec((d, n), lambda i: (0, 0)),
            pl.BlockSpec((1, d, LANES), lambda i: (i // per_batch, 0, 0)),
        ],
        out_specs=[
            pl.BlockSpec((tm, n), lambda i: (i, 0)),
            pl.BlockSpec((tm, LANES), lambda i: (i, 0)),
        ],
        out_shape=[
            jax.ShapeDtypeStruct((t, n), BF),
            jax.ShapeDtypeStruct((t, LANES), F32),
        ],
        compiler_params=pltpu.CompilerParams(
            dimension_semantics=("arbitrary",), vmem_limit_bytes=VMEM_LIMIT),
        name="in_proj",
    )(x2, g, w_main, w_ab)


def _memkv_body(mem_ref, g_ref, wk_ref, wv_ref, k_ref, v_ref):
    mn = _rmsnorm(mem_ref[0], g_ref[...]).astype(BF)
    k_ref[0] = _dot(mn, wk_ref[...]).astype(BF)
    v_ref[0] = _dot(mn, wv_ref[...]).astype(BF)


def _mem_kv(mem, g, wk, wv):
    b, m, d = mem.shape
    return pl.pallas_call(
        _memkv_body,
        grid=(b,),
        in_specs=[
            pl.BlockSpec((1, m, d), lambda i: (i, 0, 0)),
            pl.BlockSpec((1, d), lambda i: (0, 0)),
            pl.BlockSpec((d, d), lambda i: (0, 0)),
            pl.BlockSpec((d, d), lambda i: (0, 0)),
        ],
        out_specs=[
            pl.BlockSpec((1, m, d), lambda i: (i, 0, 0)),
            pl.BlockSpec((1, m, d), lambda i: (i, 0, 0)),
        ],
        out_shape=[jax.ShapeDtypeStruct((b, m, d), BF)] * 2,
        compiler_params=pltpu.CompilerParams(
            dimension_semantics=("arbitrary",), vmem_limit_bytes=VMEM_LIMIT),
        name="mem_kv",
    )(mem, g, wk, wv)


def _causal_conv(slab_ref, slab, x, w_ref, col):
    ts = x.shape[0]
    k = w_ref.shape[0]
    slab_ref[slab, HALO:HALO + ts, :] = x
    acc = x * w_ref[k - 1:k, col]
    for sh in range(1, k):
        acc = acc + slab_ref[slab, HALO - sh:HALO - sh + ts, :] * w_ref[k - 1 - sh:k - sh, col]
    slab_ref[slab, 0:HALO, :] = x[ts - HALO:ts]
    return acc


def _blockdiag2(a, b):
    za = jnp.zeros_like(a)
    return jnp.concatenate(
        [jnp.concatenate([a, za], axis=1), jnp.concatenate([za, b], axis=1)], axis=0)


def _mixer_body(qkv_ref, z_ref, scb_ref, scc_ref, sch_ref, ab_ref,
                cw_ref, alog_ref, dtb_ref, gng_ref, scw_ref, scg_ref,
                mix_ref,
                conv_s, sc_s, state, qkv_s, *, batch, heads):
    hd = GDN_HEAD_DIM
    gw = heads * hd
    nq = 3 * heads
    npair = heads // 2
    gl = 2 * heads
    bh = [(b, h) for b in range(batch) for h in range(heads)]
    bp = [(b, p) for b in range(batch) for p in range(npair)]

    @pl.when(pl.program_id(0) == 0)
    def _():
        conv_s[:, 0:HALO, :] = jnp.zeros((conv_s.shape[0], HALO, hd), F32)
        sc_s[:, 0:HALO, :] = jnp.zeros((sc_s.shape[0], HALO, hd), F32)
        state[...] = jnp.zeros_like(state)

    for b in range(batch):
        for blk in range(nq):
            col = slice(blk * hd, (blk + 1) * hd)
            x = qkv_ref[b, :, col].astype(F32)
            y = _silu(_causal_conv(conv_s, b * nq + blk, x, cw_ref, col))
            if blk < 2 * heads:
                y = y * lax.rsqrt(jnp.sum(y * y, axis=-1, keepdims=True) + EPS)
            if blk < heads:
                y = y * (hd ** -0.5)
            qkv_s[b, :, col] = y

    abp = ab_ref[0]
    for b in range(1, batch):
        abp = abp + ab_ref[b]
    g_all = -jnp.exp(alog_ref[...]) * _softplus(abp + dtb_ref[...])
    beta_all = _sigmoid(abp)
    ri = lax.broadcasted_iota(jnp.int32, (CHUNK, CHUNK), 0)
    ci = lax.broadcasted_iota(jnp.int32, (CHUNK, CHUNK), 1)
    ltri = jnp.where(ci <= ri, 1.0, 0.0).astype(BF)
    gc_all = _sum([_dot(ltri, t) for t in _split_bf16(g_all, 3)])

    r64 = lax.broadcasted_iota(jnp.int32, (CHUNK, 2 * CHUNK), 0)
    l64 = lax.broadcasted_iota(jnp.int32, (CHUNK, 2 * CHUNK), 1)
    lm = l64 % CHUNK
    causal = r64 >= lm
    strict = r64 > lm
    eye_p = r64 == lm
    lane_lo = l64 < CHUNK
    rb = lax.broadcasted_iota(jnp.int32, (2 * CHUNK, 2 * CHUNK), 0)
    lb = lax.broadcasted_iota(jnp.int32, (2 * CHUNK, 2 * CHUNK), 1)
    blockmask = (rb // CHUNK) == (lb // CHUNK)

    def bd(m):
        return jnp.where(blockmask, jnp.concatenate([m, m], axis=0), 0.0).astype(BF)

    hv = {}
    for b, h in bh:
        lane = b * gl + h
        q = qkv_s[b, :, h * hd:(h + 1) * hd]
        k = qkv_s[b, :, gw + h * hd:gw + (h + 1) * hd]
        v = qkv_s[b, :, 2 * gw + h * hd:2 * gw + (h + 1) * hd]
        gcol = gc_all[:, lane:lane + 1]
        bcol = beta_all[:, lane + heads:lane + heads + 1]
        glast = gc_all[CHUNK - 1:CHUNK, lane:lane + 1]
        egc = jnp.exp(gcol)
        kb = k * bcol
        hv[b, h] = dict(q=q, k=k, vb=v * bcol, kb=kb, kbg=kb * egc, qd=q * egc,
                        kd=k * jnp.exp(glast - gcol), gcol=gcol, glast=glast)

    kk = {}
    for b, p in bp:
        a0, a1 = hv[b, 2 * p], hv[b, 2 * p + 1]
        lhs = jnp.concatenate([
            jnp.concatenate([a0["kb"], a1["kb"]], axis=1),
            jnp.concatenate([a0["q"], a1["q"]], axis=1)], axis=0).astype(BF)
        kk[b, p] = _dot_nt(lhs, _blockdiag2(a0["k"], a1["k"]).astype(BF))

    n_mat, b_mat, attn = {}, {}, {}
    for b, p in bp:
        a0, a1 = hv[b, 2 * p], hv[b, 2 * p + 1]
        gmat = jnp.where(lane_lo, jnp.broadcast_to(a0["gcol"], (CHUNK, 2 * CHUNK)),
                         jnp.broadcast_to(a1["gcol"], (CHUNK, 2 * CHUNK)))
        grow = jnp.sum(jnp.where(eye_p, gmat, 0.0), axis=0, keepdims=True)
        decay = jnp.exp(jnp.where(causal, gmat - grow, -jnp.inf))
        a_mat = jnp.where(strict, kk[b, p][:CHUNK] * decay, 0.0)
        attn[b, p] = kk[b, p][CHUNK:] * decay
        n_mat[b, p] = -a_mat
        b_mat[b, p] = a_mat

    for b, p in bp:
        b_mat[b, p] = _dot(b_mat[b, p].astype(BF), bd(b_mat[b, p]))
    for _ in range(4):
        r = {}
        for b, p in bp:
            r[b, p] = _dot(jnp.concatenate([n_mat[b, p], b_mat[b, p]], axis=0).astype(BF), bd(b_mat[b, p]))
        for b, p in bp:
            n_mat[b, p] = n_mat[b, p] + b_mat[b, p] + r[b, p][:CHUNK]
            b_mat[b, p] = r[b, p][CHUNK:]
    r = {}
    for b, p in bp:
        r[b, p] = _dot(n_mat[b, p].astype(BF), bd(b_mat[b, p]))
    t_mat = {}
    for b, p in bp:
        t_mat[b, p] = jnp.where(eye_p, 1.0, 0.0) + n_mat[b, p] + b_mat[b, p] + r[b, p]

    uw = {}
    for b, p in bp:
        a0, a1 = hv[b, 2 * p], hv[b, 2 * p + 1]
        rhs = jnp.concatenate([_blockdiag2(a0["vb"], a1["vb"]),
                               _blockdiag2(a0["kbg"], a1["kbg"])], axis=1).astype(BF)
        uw[b, p] = _dot(t_mat[b, p].astype(BF), rhs)

    r1 = {}
    for b, h in bh:
        p, j = divmod(h, 2)
        w = uw[b, p][:, 2 * hd + j * hd:2 * hd + (j + 1) * hd]
        r1[b, h] = _dot(jnp.concatenate([w, hv[b, h]["qd"]], axis=0).astype(BF),
                        state[b * heads + h].astype(BF))
    vn = {}
    for b, h in bh:
        p, j = divmod(h, 2)
        vn[b, h] = uw[b, p][:, j * hd:(j + 1) * hd] - r1[b, h][:CHUNK]

    av = {}
    for b, p in bp:
        av[b, p] = _dot(attn[b, p].astype(BF), _blockdiag2(vn[b, 2 * p], vn[b, 2 * p + 1]).astype(BF))
    for b, h in bh:
        p, j = divmod(h, 2)
        idx = b * heads + h
        state[idx] = (state[idx] * jnp.exp(hv[b, h]["glast"])
                      + _dot_tn(hv[b, h]["kd"].astype(BF), vn[b, h].astype(BF)))
        o = r1[b, h][CHUNK:] + av[b, p][:, j * hd:(j + 1) * hd]
        zc = z_ref[b, :, h * hd:(h + 1) * hd].astype(F32)
        on = _rmsnorm(o, gng_ref[...]) * _silu(zc)
        mix_ref[b, :, h * hd:(h + 1) * hd] = on.astype(BF)

    nsc = scb_ref.shape[2] // hd
    gi = lax.broadcasted_iota(jnp.int32, (hd, hd), 0) // SC_GROUP
    gj = lax.broadcasted_iota(jnp.int32, (hd, hd), 1) // SC_GROUP
    gavg = jnp.where(gi == gj, 1.0 / SC_GROUP, 0.0).astype(BF)
    ys = []
    for b in range(batch):
        for blk in range(nsc):
            col = slice(blk * hd, (blk + 1) * hd)
            ch = scc_ref[b, :, col].astype(F32) * sch_ref[b, :, col].astype(F32)
            ys.append(scb_ref[b, :, col].astype(F32) * _causal_conv(sc_s, b * nsc + blk, ch, scw_ref, col))
    ysq = jnp.concatenate([y * y for y in ys], axis=0)
    ms = _sum([_dot(t, gavg) for t in _split_bf16(ysq, 2)])
    for i, y in enumerate(ys):
        b, blk = divmod(i, nsc)
        col = slice(blk * hd, (blk + 1) * hd)
        yn = y * lax.rsqrt(ms[i * CHUNK:(i + 1) * CHUNK] + EPS) * scg_ref[:, col]
        mix_ref[b, :, gw + blk * hd:gw + (blk + 1) * hd] = yn.astype(BF)


def _mixer(proj, ab, conv_qkv_w, alog_row, dtb_row, gdn_norm_g, conv_sc_w, sc_norm_g, *, heads):
    batch, seq, _ = proj.shape
    hd = GDN_HEAD_DIM
    gw = heads * hd
    scw = conv_sc_w.shape[1]
    ts = CHUNK
    assert gw == scw and seq % ts == 0 and heads % 2 == 0 and 2 * heads * batch <= LANES
    const = lambda s: (0, 0)

    in_specs = [
        pl.BlockSpec((batch, ts, 3 * gw), lambda s: (0, s, 0)),
        pl.BlockSpec((batch, ts, gw), lambda s: (0, s, 3)),
        pl.BlockSpec((batch, ts, scw), lambda s: (0, s, 4)),
        pl.BlockSpec((batch, ts, scw), lambda s: (0, s, 5)),
        pl.BlockSpec((batch, ts, scw), lambda s: (0, s, 6)),
        pl.BlockSpec((batch, ts, LANES), lambda s: (0, s, 0)),
        pl.BlockSpec(conv_qkv_w.shape, const),
        pl.BlockSpec((1, LANES), const),
        pl.BlockSpec((1, LANES), const),
        pl.BlockSpec((1, hd), const),
        pl.BlockSpec(conv_sc_w.shape, const),
        pl.BlockSpec((1, scw), const),
    ]
    return pl.pallas_call(
        functools.partial(_mixer_body, batch=batch, heads=heads),
        grid=(seq // ts,),
        in_specs=in_specs,
        out_specs=pl.BlockSpec((batch, ts, gw + scw), lambda s: (0, s, 0)),
        out_shape=jax.ShapeDtypeStruct((batch, seq, gw + scw), BF),
        scratch_shapes=[
            pltpu.VMEM((batch * 3 * heads, HALO + ts, hd), F32),
            pltpu.VMEM((batch * scw // hd, HALO + ts, hd), F32),
            pltpu.VMEM((batch * heads, hd, hd), F32),
            pltpu.VMEM((batch, ts, 3 * gw), F32),
        ],
        compiler_params=pltpu.CompilerParams(
            dimension_semantics=("arbitrary",), vmem_limit_bytes=VMEM_LIMIT),
        name="mixer",
    )(proj, proj, proj, proj, proj, ab, conv_qkv_w, alog_row, dtb_row, gdn_norm_g,
      conv_sc_w, sc_norm_g)


def _xattn_body(mix_ref, x_ref, wout_ref, gx_ref, wq_ref, k_ref, v_ref, wo_ref, h2_ref, *, heads):
    d = x_ref.shape[1]
    dh = d // heads
    h1 = x_ref[...] + _dot(mix_ref[...], wout_ref[...])
    hn = _rmsnorm(h1, gx_ref[...]).astype(BF)
    q = (_dot(hn, wq_ref[...]) * (dh ** -0.5)).astype(BF)
    outs = []
    for h in range(heads):
        col = slice(h * dh, (h + 1) * dh)
        s = _dot_nt(q[:, col], k_ref[0, :, col])
        m = jnp.max(s, axis=-1, keepdims=True)
        p = jnp.exp(s - m)
        p = p * (1.0 / jnp.sum(p, axis=-1, keepdims=True))
        outs.append(_dot(p.astype(BF), v_ref[0, :, col]))
    o = jnp.concatenate(outs, axis=1).astype(BF)
    h2_ref[...] = h1 + _dot(o, wo_ref[...])


def _xattn(mix, x2, w_out, gx, w_xq, kmem, vmem, w_xo, *, batch, seq, tm):
    t, d = x2.shape
    m = kmem.shape[1]
    nblk = seq // tm
    const = lambda b, s: (0, 0)
    return pl.pallas_call(
        functools.partial(_xattn_body, heads=XATTN_HEADS),
        grid=(batch, nblk),
        in_specs=[
            pl.BlockSpec((tm, d), lambda b, s: (b * nblk + s, 0)),
            pl.BlockSpec((tm, d), lambda b, s: (b * nblk + s, 0)),
            pl.BlockSpec((d, d), const),
            pl.BlockSpec((1, d), const),
            pl.BlockSpec((d, d), const),
            pl.BlockSpec((1, m, d), lambda b, s: (b, 0, 0)),
            pl.BlockSpec((1, m, d), lambda b, s: (b, 0, 0)),
            pl.BlockSpec((d, d), const),
        ],
        out_specs=pl.BlockSpec((tm, d), lambda b, s: (b * nblk + s, 0)),
        out_shape=jax.ShapeDtypeStruct((t, d), F32),
        compiler_params=pltpu.CompilerParams(
            dimension_semantics=("arbitrary", "arbitrary"), vmem_limit_bytes=VMEM_LIMIT),
        name="xattn",
    )(mix, x2, w_out, gx, w_xq, kmem, vmem, w_xo)


def _ffn_body(h_ref, gf_ref, wg_ref, wu_ref, wd_ref, gfin_ref, out_ref, acc_ref, *, ff_chunk):
    h2 = h_ref[...]
    hn = _rmsnorm(h2, gf_ref[...]).astype(BF)
    dff = wg_ref.shape[1]
    acc_ref[...] = h2
    for c in range(0, dff, ff_chunk):
        w = min(ff_chunk, dff - c)
        gate = _dot(hn, wg_ref[:, c:c + w])
        up = _dot(hn, wu_ref[:, c:c + w])
        act = (_silu(gate) * up).astype(BF)
        acc_ref[...] += _dot(act, wd_ref[c:c + w, :])
    out_ref[...] = _rmsnorm(acc_ref[...], gfin_ref[...])


def _ffn(h2, gf, wg, wu, wd, gfin, *, tm):
    t, d = h2.shape
    dff = wg.shape[1]
    const = lambda i: (0, 0)
    resident = dict(pipeline_mode=pl.Buffered(1))
    return pl.pallas_call(
        functools.partial(_ffn_body, ff_chunk=512),
        grid=(t // tm,),
        in_specs=[
            pl.BlockSpec((tm, d), lambda i: (i, 0)),
            pl.BlockSpec((1, d), const),
            pl.BlockSpec((d, dff), const, **resident),
            pl.BlockSpec((d, dff), const, **resident),
            pl.BlockSpec((dff, d), const, **resident),
            pl.BlockSpec((1, d), const),
        ],
        out_specs=pl.BlockSpec((tm, d), lambda i: (i, 0)),
        out_shape=jax.ShapeDtypeStruct((t, d), F32),
        scratch_shapes=[pltpu.VMEM((tm, d), F32)],
        compiler_params=pltpu.CompilerParams(
            dimension_semantics=("arbitrary",), vmem_limit_bytes=VMEM_LIMIT),
        name="ffn",
    )(h2, gf, wg, wu, wd, gfin)


def kernel(x, mem, norm_mix_g, w_in, conv_qkv_w, a_log, dt_bias, gdn_norm_g, conv_sc_w, sc_norm_g,
           w_out, norm_x_g, norm_mem_g, w_xq, w_xk, w_xv, w_xo, norm_ffn_g, w_gate, w_up, w_down,
           norm_final_g):
    batch, seq, d = x.shape
    depth = w_in.shape[0]
    heads = a_log.shape[1]
    gw = heads * GDN_HEAD_DIM
    gl = 2 * heads
    assert depth == 1, "the final RMSNorm is fused into the (single) layer's FFN kernel"
    l = 0
    row = lambda v: v.reshape(1, -1).astype(F32)
    h = x.reshape(batch * seq, d)

    c_gate = 4 * gw
    w_main = jnp.concatenate([w_in[l][:, :c_gate], w_in[l][:, c_gate + gl:]], axis=1).astype(BF)
    w_gate_cols = w_in[l][:, c_gate:c_gate + gl].astype(BF)
    w_ab = jnp.stack([jnp.zeros((d, LANES), BF).at[:, b * gl:(b + 1) * gl].set(w_gate_cols)
                      for b in range(batch)])

    def gate_row(v):
        one = jnp.concatenate([v.astype(F32), jnp.zeros((heads,), F32)])
        return jnp.zeros((1, LANES), F32).at[0, :batch * gl].set(jnp.tile(one, batch))

    proj, ab = _in_proj(h, row(norm_mix_g[l]), w_main, w_ab, tm=512, seq=seq)
    mix = _mixer(proj.reshape(batch, seq, -1), ab.reshape(batch, seq, LANES),
                 conv_qkv_w[l].astype(F32), gate_row(a_log[l]), gate_row(dt_bias[l]),
                 row(gdn_norm_g[l]), conv_sc_w[l].astype(F32), row(sc_norm_g[l]), heads=heads)
    kmem, vmem = _mem_kv(mem, row(norm_mem_g[l]), w_xk[l].astype(BF), w_xv[l].astype(BF))
    h = _xattn(mix.reshape(batch * seq, d), h, w_out[l].astype(BF), row(norm_x_g[l]), w_xq[l].astype(BF),
               kmem, vmem, w_xo[l].astype(BF), batch=batch, seq=seq, tm=512)
    h = _ffn(h, row(norm_ffn_g[l]), w_gate[l].astype(BF), w_up[l].astype(BF), w_down[l].astype(BF),
             row(norm_final_g), tm=512)
    return h.reshape(batch, seq, d)
```

```python
import functools

import jax
import jax.numpy as jnp
from jax import lax
from jax.experimental import pallas as pl
from jax.experimental.pallas import tpu as pltpu

EPS = 1e-6
BF = jnp.bfloat16
F32 = jnp.float32

LANES = 128
GDN_HEAD_DIM = 128
CHUNK = 64
SC_GROUP = 64
XATTN_HEADS = 4
HALO = 8
ROW_SUB = 512
TOKEN_TILE = 1024
MIXER_GROUP = 8
VMEM_LIMIT = 56 * 1024 * 1024


def _dot(a, b):
    return jnp.dot(a, b, preferred_element_type=F32)


def _dot_nt(a, b):
    return lax.dot_general(a, b, (((1,), (1,)), ((), ())), preferred_element_type=F32)


def _dot_tn(a, b):
    return lax.dot_general(a, b, (((0,), (0,)), ((), ())), preferred_element_type=F32)


def _rmsnorm(x, g):
    ms = jnp.mean(x * x, axis=-1, keepdims=True)
    return x * lax.rsqrt(ms + EPS) * g


def _sigmoid(x):
    return 1.0 / (1.0 + jnp.exp(-x))


def _silu(x):
    h = 0.5 * x
    return h * jnp.tanh(h) + h


def _softplus(x):
    return jnp.maximum(x, 0.0) + jnp.log(1.0 + jnp.exp(-jnp.abs(x)))


def _sum(xs):
    return functools.reduce(lambda a, b: a + b, xs)


def _split_bf16(x, terms):
    out = []
    for _ in range(terms - 1):
        hi = x.astype(BF)
        out.append(hi)
        x = x - hi.astype(F32)
    out.append(x.astype(BF))
    return out


def _causal_conv(strip_ref, cur, nxt, n, w_ref, col):
    k = w_ref.shape[0]
    acc = strip_ref[cur, HALO:HALO + n, :] * w_ref[k - 1:k, col]
    for sh in range(1, k):
        acc = acc + strip_ref[cur, HALO - sh:HALO - sh + n, :] * w_ref[k - 1 - sh:k - sh, col]
    strip_ref[nxt, 0:HALO, :] = strip_ref[cur, n:n + HALO, :]
    return acc


def _inproj_body(x_ref, g_ref, w_ref, wab_ref, cw_ref, scw_ref, scg_ref,
                 qkv_ref, z_ref, ysc_ref, ab_ref, conv_s, sc_s, scb_s, *, heads, per_batch):
    hd = GDN_HEAD_DIM
    gw = heads * hd
    n = ROW_SUB
    nsub = x_ref.shape[0] // n
    lane = lax.broadcasted_iota(jnp.int32, (n, hd), 1)
    group_lo = lane < SC_GROUP
    proj = lambda xn, part: _dot(xn, w_ref[:, part * gw:(part + 1) * gw])

    @pl.when(pl.program_id(0) % per_batch == 0)
    def _():
        conv_s[0:3 * heads, 0:HALO, :] = jnp.zeros((3 * heads, HALO, hd), F32)
        sc_s[0:heads, 0:HALO, :] = jnp.zeros((heads, HALO, hd), F32)

    for s in range(nsub):
        rows = slice(s * n, (s + 1) * n)
        nxt = (s + 1) % nsub
        xn = _rmsnorm(x_ref[rows, :], g_ref[...]).astype(BF)

        for part in range(3):
            res = proj(xn, part)
            for h in range(heads):
                conv_s[s * 3 * heads + part * heads + h, HALO:HALO + n, :] = res[:, h * hd:(h + 1) * hd]
        scb_s[s] = proj(xn, 4)
        ch = proj(xn, 5) * proj(xn, 6)
        for h in range(heads):
            sc_s[s * heads + h, HALO:HALO + n, :] = ch[:, h * hd:(h + 1) * hd]
        z_ref[rows, :] = proj(xn, 3).astype(BF)
        ab_ref[rows, :] = _dot(xn, wab_ref[0])

        for blk in range(3 * heads):
            col = slice(blk * hd, (blk + 1) * hd)
            y = _silu(_causal_conv(conv_s, s * 3 * heads + blk, nxt * 3 * heads + blk, n, cw_ref, col))
            if blk < 2 * heads:
                y = y * lax.rsqrt(jnp.sum(y * y, axis=-1, keepdims=True) + EPS)
            if blk < heads:
                y = y * (hd ** -0.5)
            qkv_ref[rows, col] = y.astype(BF)

        for h in range(heads):
            col = slice(h * hd, (h + 1) * hd)
            y = scb_s[s, :, col] * _causal_conv(sc_s, s * heads + h, nxt * heads + h, n, scw_ref, col)
            ysq = y * y
            lo = jnp.sum(jnp.where(group_lo, ysq, 0.0), axis=-1, keepdims=True)
            hi = jnp.sum(jnp.where(group_lo, 0.0, ysq), axis=-1, keepdims=True)
            ms = jnp.where(group_lo, lo, hi) * (1.0 / SC_GROUP)
            ysc_ref[rows, col] = (y * lax.rsqrt(ms + EPS) * scg_ref[:, col]).astype(BF)


def _in_proj(x2, g, w_main, w_ab, conv_qkv_w, conv_sc_w, sc_norm_g, *, tm, seq, heads):
    t, d = x2.shape
    n = w_main.shape[1]
    gw = heads * GDN_HEAD_DIM
    assert n == 7 * gw and conv_sc_w.shape[1] == gw and LANES == 2 * SC_GROUP and tm % ROW_SUB == 0
    per_batch = seq // tm
    nsub = tm // ROW_SUB
    const = lambda i: (0, 0)
    tile = lambda width: pl.BlockSpec((tm, width), lambda i: (i, 0))
    return pl.pallas_call(
        functools.partial(_inproj_body, heads=heads, per_batch=per_batch),
        grid=(t // tm,),
        in_specs=[
            tile(d),
            pl.BlockSpec((1, d), const),
            pl.BlockSpec((d, n), const),
            pl.BlockSpec((1, d, LANES), lambda i: (i // per_batch, 0, 0)),
            pl.BlockSpec(conv_qkv_w.shape, const),
            pl.BlockSpec(conv_sc_w.shape, const),
            pl.BlockSpec((1, gw), const),
        ],
        out_specs=[tile(3 * gw), tile(gw), tile(gw), tile(LANES)],
        out_shape=[
            jax.ShapeDtypeStruct((t, 3 * gw), BF),
            jax.ShapeDtypeStruct((t, gw), BF),
            jax.ShapeDtypeStruct((t, gw), BF),
            jax.ShapeDtypeStruct((t, LANES), F32),
        ],
        scratch_shapes=[
            pltpu.VMEM((nsub * 3 * heads, HALO + ROW_SUB, GDN_HEAD_DIM), F32),
            pltpu.VMEM((nsub * heads, HALO + ROW_SUB, GDN_HEAD_DIM), F32),
            pltpu.VMEM((nsub, ROW_SUB, gw), F32),
        ],
        compiler_params=pltpu.CompilerParams(
            dimension_semantics=("arbitrary",), vmem_limit_bytes=VMEM_LIMIT),
        name="in_proj",
    )(x2, g, w_main, w_ab, conv_qkv_w, conv_sc_w, sc_norm_g)


def _memkv_body(mem_ref, g_ref, wk_ref, wv_ref, k_ref, v_ref):
    mn = _rmsnorm(mem_ref[0], g_ref[...]).astype(BF)
    k_ref[0] = _dot(mn, wk_ref[...]).astype(BF)
    v_ref[0] = _dot(mn, wv_ref[...]).astype(BF)


def _mem_kv(mem, g, wk, wv):
    b, m, d = mem.shape
    return pl.pallas_call(
        _memkv_body,
        grid=(b,),
        in_specs=[
            pl.BlockSpec((1, m, d), lambda i: (i, 0, 0)),
            pl.BlockSpec((1, d), lambda i: (0, 0)),
            pl.BlockSpec((d, d), lambda i: (0, 0)),
            pl.BlockSpec((d, d), lambda i: (0, 0)),
        ],
        out_specs=[
            pl.BlockSpec((1, m, d), lambda i: (i, 0, 0)),
            pl.BlockSpec((1, m, d), lambda i: (i, 0, 0)),
        ],
        out_shape=[jax.ShapeDtypeStruct((b, m, d), BF)] * 2,
        compiler_params=pltpu.CompilerParams(
            dimension_semantics=("arbitrary",), vmem_limit_bytes=VMEM_LIMIT),
        name="mem_kv",
    )(mem, g, wk, wv)


def _blockdiag2(a, b):
    za = jnp.zeros_like(a)
    return jnp.concatenate(
        [jnp.concatenate([a, za], axis=1), jnp.concatenate([za, b], axis=1)], axis=0)


def _mixer_body(qkv_ref, z_ref, ab_ref, alog_ref, dtb_ref, gng_ref, o_ref, state,
                *, batch, heads, group):
    hd = GDN_HEAD_DIM
    gw = heads * hd
    npair = heads // 2
    gl = 2 * heads

    @pl.when(pl.program_id(0) == 0)
    def _():
        state[...] = jnp.zeros_like(state)

    abp = ab_ref[0]
    for b in range(1, batch):
        abp = abp + ab_ref[b]
    g_all = -jnp.exp(alog_ref[...]) * _softplus(abp + dtb_ref[...])
    beta_all = _sigmoid(abp)
    ri = lax.broadcasted_iota(jnp.int32, (CHUNK, CHUNK), 0)
    ci = lax.broadcasted_iota(jnp.int32, (CHUNK, CHUNK), 1)
    ltri = jnp.where(ci <= ri, 1.0, 0.0).astype(BF)
    gc_all = _sum([_dot(ltri, t) for t in _split_bf16(g_all, 3)])

    r64 = lax.broadcasted_iota(jnp.int32, (CHUNK, 2 * CHUNK), 0)
    l64 = lax.broadcasted_iota(jnp.int32, (CHUNK, 2 * CHUNK), 1)
    lm = l64 % CHUNK
    causal = r64 >= lm
    strict = r64 > lm
    eye_p = r64 == lm
    lane_lo = l64 < CHUNK
    rb = lax.broadcasted_iota(jnp.int32, (2 * CHUNK, 2 * CHUNK), 0)
    lb = lax.broadcasted_iota(jnp.int32, (2 * CHUNK, 2 * CHUNK), 1)
    blockmask = (rb // CHUNK) == (lb // CHUNK)

    def bd(mb):
        return jnp.where(blockmask, jnp.concatenate([mb, mb], axis=0), jnp.zeros((), BF))

    def pair_rows(slab):
        return jnp.concatenate([slab, slab], axis=0).T

    gc_rows = pair_rows(gc_all)
    beta_rows = pair_rows(beta_all)

    def pair_row(rows, lane0):
        return jnp.where(lane_lo[0:1], rows[lane0:lane0 + 1], rows[lane0 + 1:lane0 + 2])

    def delta_rule(bs):
        bh = [(b, h) for b in bs for h in range(heads)]
        bp = [(b, p) for b in bs for p in range(npair)]
        hv = {}
        for b, h in bh:
            lane = b * gl + h
            q = qkv_ref[b, :, h * hd:(h + 1) * hd]
            k = qkv_ref[b, :, gw + h * hd:gw + (h + 1) * hd]
            v = qkv_ref[b, :, 2 * gw + h * hd:2 * gw + (h + 1) * hd]
            gcol = gc_all[:, lane:lane + 1]
            bcol = beta_all[:, lane + heads:lane + heads + 1]
            glast = gc_all[CHUNK - 1:CHUNK, lane:lane + 1]
            kf = k.astype(F32)
            hv[b, h] = dict(q=q, k=k, v=v, kb=kf * bcol, qd=q.astype(F32) * jnp.exp(gcol),
                            kd=kf * jnp.exp(glast - gcol), gcol=gcol, glast=glast)

        kk, k_bd = {}, {}
        for b, p in bp:
            a0, a1 = hv[b, 2 * p], hv[b, 2 * p + 1]
            lhs = jnp.concatenate([
                jnp.concatenate([a0["kb"], a1["kb"]], axis=1).astype(BF),
                jnp.concatenate([a0["q"], a1["q"]], axis=1)], axis=0)
            k_bd[b, p] = _blockdiag2(a0["k"], a1["k"])
            kk[b, p] = _dot_nt(lhs, k_bd[b, p])

        n_mat, b_mat, attn, grow = {}, {}, {}, {}
        for b, p in bp:
            a0, a1 = hv[b, 2 * p], hv[b, 2 * p + 1]
            gmat = jnp.where(lane_lo, jnp.broadcast_to(a0["gcol"], (CHUNK, 2 * CHUNK)),
                             jnp.broadcast_to(a1["gcol"], (CHUNK, 2 * CHUNK)))
            grow[b, p] = pair_row(gc_rows, b * gl + 2 * p)
            decay = jnp.exp(jnp.where(causal, gmat - grow[b, p], -jnp.inf))
            a_mat = jnp.where(strict, kk[b, p][:CHUNK] * decay, 0.0)
            attn[b, p] = kk[b, p][CHUNK:] * decay
            n_mat[b, p] = -a_mat
            b_mat[b, p] = a_mat

        for b, p in bp:
            bb = b_mat[b, p].astype(BF)
            b_mat[b, p] = _dot(bb, bd(bb))
        for _ in range(4):
            r = {}
            for b, p in bp:
                bb = b_mat[b, p].astype(BF)
                r[b, p] = _dot(jnp.concatenate([n_mat[b, p].astype(BF), bb], axis=0), bd(bb))
            for b, p in bp:
                n_mat[b, p] = n_mat[b, p] + b_mat[b, p] + r[b, p][:CHUNK]
                b_mat[b, p] = r[b, p][CHUNK:]
        r = {}
        for b, p in bp:
            r[b, p] = _dot(n_mat[b, p].astype(BF), bd(b_mat[b, p].astype(BF)))
        t_mat = {}
        for b, p in bp:
            t_mat[b, p] = jnp.where(eye_p, 1.0, 0.0) + n_mat[b, p] + b_mat[b, p] + r[b, p]

        u, w = {}, {}
        for b, p in bp:
            a0, a1 = hv[b, 2 * p], hv[b, 2 * p + 1]
            brow = pair_row(beta_rows, b * gl + heads + 2 * p)
            u[b, p] = _dot((t_mat[b, p] * brow).astype(BF), _blockdiag2(a0["v"], a1["v"]))
            w[b, p] = _dot((t_mat[b, p] * (brow * jnp.exp(grow[b, p]))).astype(BF), k_bd[b, p])

        r1 = {}
        for b, h in bh:
            p, j = divmod(h, 2)
            r1[b, h] = _dot(jnp.concatenate([w[b, p][:, j * hd:(j + 1) * hd], hv[b, h]["qd"]],
                                            axis=0).astype(BF),
                            state[b * heads + h].astype(BF))
        vn = {}
        for b, h in bh:
            p, j = divmod(h, 2)
            vn[b, h] = u[b, p][:, j * hd:(j + 1) * hd] - r1[b, h][:CHUNK]

        av = {}
        for b, p in bp:
            av[b, p] = _dot(attn[b, p].astype(BF),
                            _blockdiag2(vn[b, 2 * p], vn[b, 2 * p + 1]).astype(BF))
        for b, h in bh:
            p, j = divmod(h, 2)
            idx = b * heads + h
            state[idx] = (state[idx] * jnp.exp(hv[b, h]["glast"])
                          + _dot_tn(hv[b, h]["kd"].astype(BF), vn[b, h].astype(BF)))
            o = r1[b, h][CHUNK:] + av[b, p][:, j * hd:(j + 1) * hd]
            zc = z_ref[b, :, h * hd:(h + 1) * hd].astype(F32)
            on = _rmsnorm(o, gng_ref[...]) * _silu(zc)
            o_ref[b, :, h * hd:(h + 1) * hd] = on.astype(BF)

    for g0 in range(0, batch, group):
        delta_rule(range(g0, g0 + group))


def _mixer(qkv, z, ab, alog_row, dtb_row, gdn_norm_g, *, heads):
    batch, seq, _ = qkv.shape
    hd = GDN_HEAD_DIM
    gw = heads * hd
    ts = CHUNK
    group = min(MIXER_GROUP, batch)
    assert seq % ts == 0 and heads % 2 == 0 and 2 * heads * batch <= LANES and batch % group == 0
    const = lambda s: (0, 0)
    tile = lambda width: pl.BlockSpec((batch, ts, width), lambda s: (0, s, 0))
    return pl.pallas_call(
        functools.partial(_mixer_body, batch=batch, heads=heads, group=group),
        grid=(seq // ts,),
        in_specs=[
            tile(3 * gw), tile(gw), tile(LANES),
            pl.BlockSpec((1, LANES), const),
            pl.BlockSpec((1, LANES), const),
            pl.BlockSpec((1, hd), const),
        ],
        out_specs=tile(gw),
        out_shape=jax.ShapeDtypeStruct((batch, seq, gw), BF),
        scratch_shapes=[pltpu.VMEM((batch * heads, hd, hd), F32)],
        compiler_params=pltpu.CompilerParams(
            dimension_semantics=("arbitrary",), vmem_limit_bytes=VMEM_LIMIT),
        name="mixer",
    )(qkv, z, ab, alog_row, dtb_row, gdn_norm_g)


def _xattn_body(og_ref, ysc_ref, x_ref, wout_ref, gx_ref, wq_ref, k_ref, v_ref, wo_ref, h2_ref, *, heads):
    tm, d = x_ref.shape
    dh = d // heads
    gw = og_ref.shape[1]
    subs = [slice(r, r + ROW_SUB) for r in range(0, tm, ROW_SUB)]
    cols = [slice(h * dh, (h + 1) * dh) for h in range(heads)]
    h1 = [x_ref[r, :] + _dot(og_ref[r, :], wout_ref[0:gw, :]) + _dot(ysc_ref[r, :], wout_ref[gw:, :])
          for r in subs]
    q = [(_dot(_rmsnorm(a, gx_ref[...]).astype(BF), wq_ref[...]) * (dh ** -0.5)).astype(BF) for a in h1]
    s = [[_dot_nt(qi[:, c], k_ref[0, :, c]) for c in cols] for qi in q]
    o = []
    for si in s:
        oi = []
        for sh, c in zip(si, cols):
            p = jnp.exp(sh - jnp.max(sh, axis=-1, keepdims=True))
            p = p * (1.0 / jnp.sum(p, axis=-1, keepdims=True))
            oi.append(_dot(p.astype(BF), v_ref[0, :, c]))
        o.append(oi)
    for r, a, oi in zip(subs, h1, o):
        h2_ref[r, :] = a + _sum([_dot(oh.astype(BF), wo_ref[c, :]) for oh, c in zip(oi, cols)])


def _xattn(og, ysc, x2, w_out, gx, w_xq, kmem, vmem, w_xo, *, batch, seq, tm):
    t, d = x2.shape
    m = kmem.shape[1]
    nblk = seq // tm
    const = lambda b, s: (0, 0)
    tile = lambda width: pl.BlockSpec((tm, width), lambda b, s: (b * nblk + s, 0))
    return pl.pallas_call(
        functools.partial(_xattn_body, heads=XATTN_HEADS),
        grid=(batch, nblk),
        in_specs=[
            tile(og.shape[1]), tile(ysc.shape[1]), tile(d),
            pl.BlockSpec((d, d), const),
            pl.BlockSpec((1, d), const),
            pl.BlockSpec((d, d), const),
            pl.BlockSpec((1, m, d), lambda b, s: (b, 0, 0)),
            pl.BlockSpec((1, m, d), lambda b, s: (b, 0, 0)),
            pl.BlockSpec((d, d), const),
        ],
        out_specs=tile(d),
        out_shape=jax.ShapeDtypeStruct((t, d), F32),
        compiler_params=pltpu.CompilerParams(
            dimension_semantics=("arbitrary", "arbitrary"), vmem_limit_bytes=VMEM_LIMIT),
        name="xattn",
    )(og, ysc, x2, w_out, gx, w_xq, kmem, vmem, w_xo)


def _ffn_body(h_ref, gf_ref, wg_ref, wu_ref, wd_ref, gfin_ref, out_ref, *, ff_chunk):
    tm = h_ref.shape[0]
    dff = wg_ref.shape[1]
    chunks = [slice(c, min(c + ff_chunk, dff)) for c in range(0, dff, ff_chunk)]
    for r0 in range(0, tm, ROW_SUB):
        rows = slice(r0, r0 + ROW_SUB)
        h2 = h_ref[rows, :]
        hn = _rmsnorm(h2, gf_ref[...]).astype(BF)
        out_ref[rows, :] = h2
        act = None
        for c in chunks + [None]:
            nxt = None
            if c is not None:
                nxt = (_dot(hn, wg_ref[:, c]), _dot(hn, wu_ref[:, c]), c)
            if act is not None:
                out_ref[rows, :] += _dot(act[0], wd_ref[act[1], :])
            act = None if nxt is None else ((_silu(nxt[0]) * nxt[1]).astype(BF), nxt[2])
        out_ref[rows, :] = _rmsnorm(out_ref[rows, :], gfin_ref[...])


def _ffn(h2, gf, wg, wu, wd, gfin, *, tm):
    t, d = h2.shape
    dff = wg.shape[1]
    const = lambda i: (0, 0)
    resident = dict(pipeline_mode=pl.Buffered(1))
    return pl.pallas_call(
        functools.partial(_ffn_body, ff_chunk=512),
        grid=(t // tm,),
        in_specs=[
            pl.BlockSpec((tm, d), lambda i: (i, 0)),
            pl.BlockSpec((1, d), const),
            pl.BlockSpec((d, dff), const, **resident),
            pl.BlockSpec((d, dff), const, **resident),
            pl.BlockSpec((dff, d), const, **resident),
            pl.BlockSpec((1, d), const),
        ],
        out_specs=pl.BlockSpec((tm, d), lambda i: (i, 0)),
        out_shape=jax.ShapeDtypeStruct((t, d), F32),
        compiler_params=pltpu.CompilerParams(
            dimension_semantics=("arbitrary",), vmem_limit_bytes=VMEM_LIMIT),
        name="ffn",
    )(h2, gf, wg, wu, wd, gfin)


def kernel(x, mem, norm_mix_g, w_in, conv_qkv_w, a_log, dt_bias, gdn_norm_g, conv_sc_w, sc_norm_g,
           w_out, norm_x_g, norm_mem_g, w_xq, w_xk, w_xv, w_xo, norm_ffn_g, w_gate, w_up, w_down,
           norm_final_g):
    batch, seq, d = x.shape
    depth = w_in.shape[0]
    heads = a_log.shape[1]
    gw = heads * GDN_HEAD_DIM
    gl = 2 * heads
    assert depth == 1, "the final RMSNorm is fused into the (single) layer's FFN kernel"
    l = 0
    row = lambda v: v.reshape(1, -1).astype(F32)
    h = x.reshape(batch * seq, d)

    c_gate = 4 * gw
    w_main = jnp.concatenate([w_in[l][:, :c_gate], w_in[l][:, c_gate + gl:]], axis=1).astype(BF)
    w_gate_cols = w_in[l][:, c_gate:c_gate + gl].astype(BF)
    w_ab = jnp.stack([jnp.zeros((d, LANES), BF).at[:, b * gl:(b + 1) * gl].set(w_gate_cols)
                      for b in range(batch)])

    def gate_row(v):
        one = jnp.concatenate([v.astype(F32), jnp.zeros((heads,), F32)])
        return jnp.zeros((1, LANES), F32).at[0, :batch * gl].set(jnp.tile(one, batch))

    qkv, z, ysc, ab = _in_proj(h, row(norm_mix_g[l]), w_main, w_ab, conv_qkv_w[l].astype(F32),
                               conv_sc_w[l].astype(F32), row(sc_norm_g[l]),
                               tm=TOKEN_TILE, seq=seq, heads=heads)
    og = _mixer(qkv.reshape(batch, seq, -1), z.reshape(batch, seq, -1), ab.reshape(batch, seq, LANES),
                gate_row(a_log[l]), gate_row(dt_bias[l]), row(gdn_norm_g[l]), heads=heads)
    kmem, vmem = _mem_kv(mem, row(norm_mem_g[l]), w_xk[l].astype(BF), w_xv[l].astype(BF))
    h = _xattn(og.reshape(batch * seq, gw), ysc, h, w_out[l].astype(BF), row(norm_x_g[l]),
               w_xq[l].astype(BF), kmem, vmem, w_xo[l].astype(BF), batch=batch, seq=seq, tm=TOKEN_TILE)
    h = _ffn(h, row(norm_ffn_g[l]), w_gate[l].astype(BF), w_up[l].astype(BF), w_down[l].astype(BF),
             row(norm_final_g), tm=TOKEN_TILE)
    return h.reshape(batch, seq, d)
```
